```python
import jax, jax.numpy as jnp
from jax import lax
import numpy as np

D_MODEL = 1024
BATCH = 32
SEQ = 2048
DEPTH = 1
DEC_BATCH = 2
DEC_SEQ = 8192
PAST_LEN = 128

N_META = 16
GRID_W = 64
MIX_W = D_MODEL
W_M = MIX_W // 2
H_M = 4
DH_M = W_M // H_M
W_A = MIX_W - W_M
H_A = 8
DH_A = W_A // H_A
WIN_H = 8
WIN_W = 16
CHUNK = 64
CONV_K = 5
IN_COLS = 5 * W_M + 4 * H_M + 4 * W_A
EPS = 1e-6
NEG_BIG = -1e30

kernel_name = 'hymba_mlstm_natten_encoder'


def rms_norm(x, g):
    xf = x.astype(jnp.float32)
    y = xf * lax.rsqrt(jnp.mean(xf * xf, axis=-1, keepdims=True) + EPS)
    return (y * g.astype(jnp.float32)).astype(x.dtype)


def centred_dwconv(x, w):
    return lax.conv_general_dilated(
        x, w[:, None, :].astype(x.dtype), window_strides=(1,),
        padding=[(CONV_K // 2, CONV_K // 2)],
        dimension_numbers=('NWC', 'WIO', 'NWC'), feature_group_count=x.shape[-1])


def _mlstm_scan(q, k, v, li, lf):
    B, H, Lp, d = q.shape
    nc = Lp // CHUNK
    tri = jnp.tril(jnp.ones((CHUNK, CHUNK), dtype=bool))

    def to_chunks(a):
        return jnp.moveaxis(a.reshape(a.shape[:2] + (nc, CHUNK) + a.shape[3:]), 2, 0)

    def step(carry, inp):
        C, n, m = carry
        qc, kc, vc, lic, lfc = inp
        b = jnp.cumsum(lfc, axis=-1)
        D = jnp.where(tri, b[..., :, None] - b[..., None, :] + lic[..., None, :], -jnp.inf)
        g = b + m[..., None]
        m_j = jnp.maximum(g, jnp.max(D, axis=-1))
        S = jnp.einsum('bhjd,bhsd->bhjs', qc, kc) * jnp.exp(D - m_j[..., None])
        w_inter = jnp.exp(g - m_j)
        num = jnp.einsum('bhjs,bhsd->bhjd', S, vc) + w_inter[..., None] * jnp.einsum('bhed,bhjd->bhje', C, qc)
        den = jnp.sum(S, axis=-1) + w_inter * jnp.einsum('bhd,bhjd->bhj', n, qc)
        h = num / jnp.maximum(jnp.abs(den), jnp.exp(-m_j))[..., None]
        bL = b[..., -1]
        w_s = bL[..., None] - b + lic
        m_new = jnp.maximum(bL + m, jnp.max(w_s, axis=-1))
        decay = jnp.exp(bL + m - m_new)
        ws = jnp.exp(w_s - m_new[..., None])
        C_new = decay[..., None, None] * C + jnp.einsum('bhs,bhse,bhsd->bhed', ws, vc, kc)
        n_new = decay[..., None] * n + jnp.einsum('bhs,bhsd->bhd', ws, kc)
        return (C_new, n_new, m_new), h

    init = (jnp.zeros((B, H, d, d), jnp.float32), jnp.zeros((B, H, d), jnp.float32),
            jnp.zeros((B, H), jnp.float32))
    _, h = lax.scan(step, init, (to_chunks(q), to_chunks(k), to_chunks(v), to_chunks(li), to_chunks(lf)))
    return jnp.moveaxis(h, 0, 2).reshape(B, H, Lp, d)


def mlstm_bidir(q, k, v, gates):
    B, L, H, d = q.shape
    pad = CHUNK - N_META

    def prep(a):
        return jnp.pad(jnp.swapaxes(a.astype(jnp.float32), 1, 2), ((0, 0), (0, 0), (pad, 0), (0, 0)))

    def pad_gate(a, value):
        return jnp.pad(a, ((0, 0), (0, 0), (pad, 0)), constant_values=value)

    qh, kh, vh = prep(q), prep(k * d ** -0.5), prep(v)
    g = jnp.moveaxis(gates.reshape(B, L, 4, H), 1, -1)
    li_f = pad_gate(g[:, 0], NEG_BIG)
    lf_f = pad_gate(jax.nn.log_sigmoid(g[:, 1]), 0.0)
    li_b = pad_gate(g[:, 2], NEG_BIG)
    lf_b = pad_gate(jax.nn.log_sigmoid(g[:, 3]), 0.0)
    flip = lambda a: jnp.flip(a, axis=2)
    h_f = _mlstm_scan(qh, kh, vh, li_f, lf_f)
    h_b = flip(_mlstm_scan(flip(qh), flip(kh), flip(vh), flip(li_b), flip(lf_b)))
    return jnp.swapaxes((h_f + h_b)[:, :, pad:], 1, 2)


def neighbourhood_attention(q, k, v, rpb):
    L = q.shape[0]
    T = L - N_META
    rows = T // GRID_W
    kh = min(WIN_H, rows)
    scale = DH_A ** -0.5
    qm, km, vm = q[:N_META], k[:N_META], v[:N_META]
    qr = q[N_META:].reshape(rows, GRID_W, H_A, DH_A)
    kr = k[N_META:].reshape(rows, GRID_W, H_A, DH_A)
    vr = v[N_META:].reshape(rows, GRID_W, H_A, DH_A)
    r = jnp.arange(rows)
    rs = jnp.clip(r - kh // 2, 0, rows - kh)
    key_rows = rs[:, None] + jnp.arange(kh)[None, :]
    kg, vg = kr[key_rows], vr[key_rows]
    c = jnp.arange(GRID_W)
    cs = jnp.clip(c - WIN_W // 2, 0, GRID_W - WIN_W)
    allowed = (c[None, :] >= cs[:, None]) & (c[None, :] < cs[:, None] + WIN_W)
    row_idx = key_rows - r[:, None] + (WIN_H - 1)
    col_idx = jnp.clip(c[None, :] - c[:, None] + (WIN_W - 1), 0, 2 * WIN_W - 2)
    bias = rpb[:, row_idx[:, None, :, None], col_idx[None, :, None, :]].astype(jnp.float32)
    s_win = jnp.einsum('rqhd,rkwhd->hrqkw', qr, kg).astype(jnp.float32) * scale + bias
    s_win = jnp.where(allowed[None, None, :, None, :], s_win, -jnp.inf)
    s_meta = jnp.einsum('rqhd,mhd->hrqm', qr, km).astype(jnp.float32) * scale
    p = jax.nn.softmax(jnp.concatenate([s_win.reshape(H_A, rows, GRID_W, kh * GRID_W), s_meta], axis=-1), axis=-1)
    p = p.astype(v.dtype)
    pw = p[..., :kh * GRID_W].reshape(H_A, rows, GRID_W, kh, GRID_W)
    pm = p[..., kh * GRID_W:]
    out_r = jnp.einsum('hrqkw,rkwhd->rqhd', pw, vg) + jnp.einsum('hrqm,mhd->rqhd', pm, vm)
    p_mm = jax.nn.softmax(jnp.einsum('qhd,mhd->hqm', qm, km).astype(jnp.float32) * scale, axis=-1).astype(v.dtype)
    out_m = jnp.einsum('hqm,mhd->qhd', p_mm, vm)
    return jnp.concatenate([out_m, out_r.reshape(T, H_A, DH_A)], axis=0)


def hybrid_layer(h, norm_g, w_in, b_gate, conv_w, mlstm_norm_g, q_norm_g, k_norm_g, rpb, w_out):
    B, L, _ = h.shape
    xn = rms_norm(h, norm_g)
    proj = xn @ w_in
    cuts = np.cumsum([W_M] * 5 + [4 * H_M] + [W_A] * 4)[:-1].tolist()
    q_m, k_m, v_m, o_m, z_m, gates, q_a, k_a, v_a, z_a = jnp.split(proj, cuts, axis=-1)
    qk_m = jax.nn.silu(centred_dwconv(jnp.concatenate([q_m, k_m], axis=-1), conv_w))
    q_m, k_m = jnp.split(qk_m, 2, axis=-1)
    heads_m = lambda a: a.reshape(B, L, H_M, DH_M)
    gates = gates.astype(jnp.float32) + b_gate.astype(jnp.float32)
    h_m = mlstm_bidir(heads_m(q_m), heads_m(k_m), heads_m(v_m), gates)
    h_m = jax.nn.sigmoid(heads_m(o_m).astype(jnp.float32)) * h_m
    h_m = rms_norm(h_m, mlstm_norm_g.reshape(H_M, DH_M))
    y_m = h_m.reshape(B, L, W_M).astype(h.dtype) * jax.nn.silu(z_m)
    heads_a = lambda a: a.reshape(B, L, H_A, DH_A)
    qa = rms_norm(heads_a(q_a), q_norm_g)
    ka = rms_norm(heads_a(k_a), k_norm_g)
    va = heads_a(v_a)
    a = lax.map(lambda t: neighbourhood_attention(t[0], t[1], t[2], rpb), (qa, ka, va))
    y_a = a.reshape(B, L, W_A).astype(h.dtype) * jax.nn.silu(z_a)
    return h + jnp.concatenate([y_m, y_a], axis=-1) @ w_out


def run_trunk(x, meta_tokens, norm_g, w_in, b_gate, conv_w, mlstm_norm_g, q_norm_g, k_norm_g, rpb, w_out):
    B = x.shape[0]
    meta = jnp.broadcast_to(meta_tokens[None].astype(x.dtype), (B, N_META, D_MODEL))
    h = jnp.concatenate([meta, x], axis=1)
    for l in range(DEPTH):
        h = hybrid_layer(h, norm_g[l], w_in[l], b_gate[l], conv_w[l], mlstm_norm_g[l],
                         q_norm_g[l], k_norm_g[l], rpb[l], w_out[l])
    return h[:, N_META:]


def setup_inputs(seed: int = 0) -> dict:
    key = jax.random.key(seed)
    ks = jax.random.split(key, 12)
    nrm = jax.random.normal
    f_bias = jnp.linspace(3.0, 6.0, H_M)
    gate_offset = jnp.stack([jnp.zeros(H_M), f_bias, jnp.zeros(H_M), f_bias]).reshape(4 * H_M)
    return {
        'x_prompt': nrm(ks[0], (BATCH, SEQ, D_MODEL), jnp.float32),
        'x_sample': nrm(ks[1], (DEC_BATCH, DEC_SEQ, D_MODEL), jnp.float32),
        'meta_tokens': nrm(ks[2], (N_META, D_MODEL), jnp.float32),
        'norm_g': 1.0 + 0.02 * nrm(ks[3], (DEPTH, D_MODEL), jnp.float32),
        'w_in': nrm(ks[4], (DEPTH, D_MODEL, IN_COLS), jnp.float32) * D_MODEL ** -0.5,
        'b_gate': gate_offset + 0.1 * nrm(ks[5], (DEPTH, 4 * H_M), jnp.float32),
        'conv_w': nrm(ks[6], (DEPTH, CONV_K, 2 * W_M), jnp.float32) * CONV_K ** -0.5,
        'mlstm_norm_g': 1.0 + 0.02 * nrm(ks[7], (DEPTH, W_M), jnp.float32),
        'q_norm_g': 1.0 + 0.02 * nrm(ks[8], (DEPTH, DH_A), jnp.float32),
        'k_norm_g': 1.0 + 0.02 * nrm(ks[9], (DEPTH, DH_A), jnp.float32),
        'rpb': 0.1 * nrm(ks[10], (DEPTH, H_A, 2 * WIN_H - 1, 2 * WIN_W - 1), jnp.float32),
        'w_out': nrm(ks[11], (DEPTH, MIX_W, D_MODEL), jnp.float32) * MIX_W ** -0.5,
    }


def reference(x_prompt, x_sample, meta_tokens, norm_g, w_in, b_gate, conv_w, mlstm_norm_g,
              q_norm_g, k_norm_g, rpb, w_out):
    y_prompt = run_trunk(x_prompt, meta_tokens, norm_g, w_in, b_gate, conv_w, mlstm_norm_g,
                         q_norm_g, k_norm_g, rpb, w_out)
    y_sample = run_trunk(x_sample, meta_tokens, norm_g, w_in, b_gate, conv_w, mlstm_norm_g,
                         q_norm_g, k_norm_g, rpb, w_out)
    return (y_prompt, y_sample)
```

```python
import functools

import jax
import jax.numpy as jnp
from jax import lax
from jax.experimental import pallas as pl
from jax.experimental.pallas import tpu as pltpu

D_MODEL = 1024
N_META = 16
GRID_W = 64
W_M = 512
H_M = 4
DH_M = 128
W_A = 512
H_A = 8
DH_A = 64
WIN_H = 8
WIN_W = 16
CONV_K = 5
EPS = 1e-6
NEG_BIG = -1e30

C_QK, C_V, C_O, C_Z, C_G, C_QA, C_KA, C_VA, C_ZA, C_END = (
    0, 1024, 1536, 2048, 2560, 2688, 3200, 3712, 4224, 4736)
N_GATE = 4 * H_M
LANES = 128
TM = 512
HALO = 16
LC = 256
ROWS_PER_STEP = 8
VMEM_LIMIT = 56 * 1024 * 1024

F32 = jnp.float32
BF16 = jnp.bfloat16


def _dot(a, b):
    return jnp.dot(a, b, preferred_element_type=F32)


def _dot_nt(a, b):
    return lax.dot_general(a, b, (((1,), (1,)), ((), ())), preferred_element_type=F32)


def _dot_tn(a, b):
    return lax.dot_general(a, b, (((0,), (0,)), ((), ())), preferred_element_type=F32)


def _sigmoid(x):
    return 1.0 / (1.0 + jnp.exp(-x))


def _silu(x):
    return x * _sigmoid(x)


def _log_sigmoid(x):
    return jnp.minimum(x, 0.0) - jnp.log1p(jnp.exp(-jnp.abs(x)))


def _rms(x, g):
    return x * lax.rsqrt(jnp.mean(x * x, axis=-1, keepdims=True) + EPS) * g


def _gate_post(g, axis):
    ch = lax.broadcasted_iota(jnp.int32, g.shape, axis)
    is_f = ((ch >= 4) & (ch < 8)) | ((ch >= 12) & (ch < 16))
    return jnp.where(is_f, _log_sigmoid(g), g)


def _head_norm(t, bd, g, scale):
    ssq = _dot((t * t).astype(BF16), bd)
    return t * lax.rsqrt(ssq * (1.0 / DH_A) + EPS) * g * scale


def _conv_silu(src_ref, row0, rows, cols, cw):
    acc = None
    for j in range(CONV_K):
        term = src_ref[pl.ds(row0 + j, rows), cols] * cw[j:j + 1, :]
        acc = term if acc is None else acc + term
    return _silu(acc)


def _metaproj_kernel(mt_ref, ng_ref, w_ref, bgr_ref, kng_ref, bd_ref,
                     pre_ref, mv_ref, mg_ref, mka_ref, mva_ref):
    xn = _rms(mt_ref[...], ng_ref[...]).astype(BF16)
    pre_ref[...] = _dot(xn, w_ref[:, C_QK:C_V])
    mv_ref[...] = _dot(xn, w_ref[:, C_V:C_O]).astype(BF16)
    mg_ref[...] = _gate_post(_dot(xn, w_ref[:, C_G:C_QA]) + bgr_ref[...], 1)
    tk = _dot(xn, w_ref[:, C_KA:C_VA])
    mka_ref[...] = _head_norm(tk, bd_ref[...], kng_ref[...], 1.0).astype(BF16)
    mva_ref[...] = _dot(xn, w_ref[:, C_VA:C_ZA]).astype(BF16)


def _inproj_kernel(tiles_per_seq,
                   xl_ref, x_ref, xr_ref, ng_ref, w_ref, wgt_ref, mpre_ref, bgr_ref, bgc_ref,
                   cw_ref, qng_ref, kng_ref, bd_ref,
                   q_ref, k_ref, v_ref, o_ref, z_ref, g_ref, gt_ref,
                   qa_ref, ka_ref, va_ref, za_ref, mk_ref,
                   xn_scr, qk_scr, mc_scr):
    t = lax.rem(pl.program_id(0), tiles_per_seq)
    ng = ng_ref[...]
    xn_scr[0:HALO, :] = _rms(xl_ref[...], ng).astype(BF16)
    xn_scr[HALO:HALO + TM, :] = _rms(x_ref[...], ng).astype(BF16)
    xn_scr[HALO + TM:, :] = _rms(xr_ref[...], ng).astype(BF16)

    xe = xn_scr[...]
    qk_scr[:, 0:W_M] = _dot(xe, w_ref[:, 0:W_M])
    qk_scr[:, W_M:2 * W_M] = _dot(xe, w_ref[:, W_M:2 * W_M])

    @pl.when(t == 0)
    def _():
        qk_scr[0:HALO, :] = mpre_ref[...]

    @pl.when(t == tiles_per_seq - 1)
    def _():
        qk_scr[HALO + TM:, :] = jnp.zeros((HALO, 2 * W_M), F32)

    cw = cw_ref[...]
    q_ref[...] = _conv_silu(qk_scr, HALO - 2, TM, slice(0, W_M), cw[:, 0:W_M]).astype(BF16)
    kc = _conv_silu(qk_scr, HALO - 2, TM, slice(W_M, 2 * W_M), cw[:, W_M:])
    k_ref[...] = (kc * DH_M ** -0.5).astype(BF16)

    @pl.when(t == 0)
    def _():
        mc_scr[0:8, :] = jnp.zeros((8, W_M), F32)
        mc_scr[8:8 + N_META, :] = mpre_ref[:, W_M:]
        mc_scr[8 + N_META:, :] = qk_scr[HALO:HALO + 8, W_M:]
        mkc = _conv_silu(mc_scr, 8 - 2, N_META, slice(None), cw[:, W_M:])
        mk_ref[0] = mkc * DH_M ** -0.5

    xm = xn_scr[HALO:HALO + TM, :]
    v_ref[...] = _dot(xm, w_ref[:, C_V:C_O]).astype(BF16)
    o_ref[...] = _dot(xm, w_ref[:, C_O:C_Z]).astype(BF16)
    z_ref[...] = _dot(xm, w_ref[:, C_Z:C_G]).astype(BF16)
    g_ref[...] = _gate_post(_dot(xm, w_ref[:, C_G:C_QA]) + bgr_ref[...], 1)
    gt_ref[...] = _gate_post(_dot_nt(wgt_ref[...], xm) + bgc_ref[:, 0:1], 0)
    bd = bd_ref[...]
    tq = _dot(xm, w_ref[:, C_QA:C_KA])
    qa_ref[...] = _head_norm(tq, bd, qng_ref[...], DH_A ** -0.5).astype(BF16)
    tk = _dot(xm, w_ref[:, C_KA:C_VA])
    ka_ref[...] = _head_norm(tk, bd, kng_ref[...], 1.0).astype(BF16)
    va_ref[...] = _dot(xm, w_ref[:, C_VA:C_ZA]).astype(BF16)
    za_ref[...] = _dot(xm, w_ref[:, C_ZA:C_END]).astype(BF16)


def _split2(x):
    hi = x.astype(BF16)
    return hi, (x - hi.astype(F32)).astype(BF16)


def _tri_cols(tri, g):
    hi, mid = _split2(g)
    r = _dot(tri, jnp.concatenate([hi, mid], axis=1))
    return r[:, :LANES] + r[:, LANES:]


def _tri_rows(gt, tri):
    hi, mid = _split2(gt)
    r = _dot(jnp.concatenate([hi, mid], axis=0), tri)
    return r[:N_GATE] + r[N_GATE:]


def _state_update(st_ref, m_ref, d, h, kh, vaug, bc, lic, tot, mprev, mst):
    wsc = tot - bc + lic
    mnew = jnp.maximum(tot + mprev, jnp.max(wsc, axis=0, keepdims=True))
    ws = jnp.exp(wsc - mnew)
    wsv = (ws * vaug.astype(F32)).astype(BF16)
    upd = _dot_tn(kh, wsv)
    if mst is not None:
        upd = jnp.exp(tot + mprev - mnew) * mst + upd
    st_ref[d, h] = upd
    m_ref[d, h] = jnp.broadcast_to(mnew, (8, LANES))


def _mlstm_kernel(qf_ref, kf_ref, vf_ref, gf_ref, gtf_ref,
                  qb_ref, kb_ref, vb_ref, gb_ref, gtb_ref,
                  mk_ref, mv_ref, mg_ref,
                  hf_ref, hb_ref, st_ref, m_ref):
    j = pl.program_id(1)
    ones_l = jnp.ones((LC, LANES), BF16)

    @pl.when(j == 0)
    def _():
        st_ref[1] = jnp.zeros((H_M, DH_M, 2 * LANES), F32)
        m_ref[1] = jnp.zeros((H_M, 8, LANES), F32)
        ri = lax.broadcasted_iota(jnp.int32, (N_META, N_META), 0)
        ci = lax.broadcasted_iota(jnp.int32, (N_META, N_META), 1)
        mg = mg_ref[...]
        bc_all = _tri_cols(jnp.where(ri >= ci, 1.0, 0.0).astype(BF16), mg)
        ones_m = jnp.ones((N_META, LANES), BF16)
        zero = jnp.zeros((1, 1), F32)
        for h in range(H_M):
            cs = slice(h * DH_M, (h + 1) * DH_M)
            bc = bc_all[:, H_M + h:H_M + h + 1]
            vaug = jnp.concatenate([mv_ref[:, cs], ones_m], axis=1)
            _state_update(st_ref, m_ref, 0, h, mk_ref[0, :, cs].astype(BF16), vaug,
                          bc, mg[:, h:h + 1], bc[N_META - 1:N_META], zero, None)

    ri = lax.broadcasted_iota(jnp.int32, (LC, LC), 0)
    ci = lax.broadcasted_iota(jnp.int32, (LC, LC), 1)
    low = ri >= ci
    upp = ri <= ci
    low_b = jnp.where(low, 1.0, 0.0).astype(BF16)
    upp_b = jnp.where(upp, 1.0, 0.0).astype(BF16)

    dirs = (
        (0, qf_ref, kf_ref, vf_ref, gf_ref, gtf_ref, hf_ref, low, low_b, upp_b, 0, LC - 1),
        (1, qb_ref, kb_ref, vb_ref, gb_ref, gtb_ref, hb_ref, upp, upp_b, low_b, 2 * H_M, 0),
    )
    for d, q_ref, k_ref, v_ref, g_ref, gt_ref, h_ref, mask, tri_c, tri_r, ch0, last in dirs:
        g = g_ref[...]
        gt = gt_ref[...]
        bc_all = _tri_cols(tri_c, g)
        br_all = _tri_rows(gt, tri_r)
        for h in range(H_M):
            cs = slice(h * DH_M, (h + 1) * DH_M)
            ci_, cf_ = ch0 + h, ch0 + H_M + h
            qh, kh, vh = q_ref[:, cs], k_ref[:, cs], v_ref[:, cs]
            bc = bc_all[:, cf_:cf_ + 1]
            lic = g[:, ci_:ci_ + 1]
            ur = gt[ci_:ci_ + 1, :] - br_all[cf_:cf_ + 1, :]
            dm = jnp.where(mask, bc + ur, -jnp.inf)
            mprev = m_ref[d, h][0:1, 0:1]
            gc = bc + mprev
            mj = jnp.maximum(gc, jnp.max(dm, axis=1, keepdims=True))
            s = (_dot_nt(qh, kh) * jnp.exp(dm - mj)).astype(BF16)
            vaug = jnp.concatenate([vh, ones_l], axis=1)
            mst = st_ref[d, h]
            tt = _dot(s, vaug) + jnp.exp(gc - mj) * _dot(qh, mst.astype(BF16))
            hout = tt[:, :LANES] / jnp.maximum(jnp.abs(tt[:, LANES:]), jnp.exp(-mj))
            h_ref[:, cs] = hout.astype(h_ref.dtype)
            _state_update(st_ref, m_ref, d, h, kh, vaug, bc, lic, bc[last:last + 1], mprev, mst)


def _attn_kernel(rows, q_ref, z_ref, k_ref, v_ref, mk_ref, mv_ref, bias_ref, y_ref):
    jb = pl.program_id(1)
    lo = lax.broadcasted_iota(jnp.int32, (GRID_W, LANES), 1) < DH_A
    ones_w = jnp.ones((WIN_H * GRID_W, LANES), BF16)
    ones_m = jnp.ones((N_META, LANES), BF16)

    def row_body(rr, carry):
        r = jb * ROWS_PER_STEP + rr
        rs = jnp.clip(r - WIN_H // 2, 0, rows - WIN_H)
        delta = rs - r + (WIN_H - 1)
        koff = pl.multiple_of(rs * GRID_W, GRID_W)
        qoff = pl.multiple_of(rr * GRID_W, GRID_W)
        for p in range(H_A // 2):
            cs = slice(p * LANES, (p + 1) * LANES)
            qp = q_ref[pl.ds(qoff, GRID_W), cs]
            zero = jnp.zeros_like(qp)
            qs = jnp.concatenate([jnp.where(lo, qp, zero), jnp.where(lo, zero, qp)], axis=0)
            s = _dot_nt(qs, k_ref[pl.ds(koff, WIN_H * GRID_W), cs]) + bias_ref[p, delta]
            sm = _dot_nt(qs, mk_ref[:, cs])
            mx = jnp.maximum(jnp.max(s, axis=1, keepdims=True), jnp.max(sm, axis=1, keepdims=True))
            e = jnp.exp(s - mx).astype(BF16)
            em = jnp.exp(sm - mx).astype(BF16)
            vaug = jnp.concatenate([v_ref[pl.ds(koff, WIN_H * GRID_W), cs], ones_w], axis=1)
            mvaug = jnp.concatenate([mv_ref[:, cs], ones_m], axis=1)
            o = _dot(e, vaug) + _dot(em, mvaug)
            on = o[:, :LANES] / o[:, LANES:]
            out = jnp.where(lo, on[0:GRID_W], on[GRID_W:])
            zp = z_ref[pl.ds(qoff, GRID_W), cs].astype(F32)
            y_ref[pl.ds(qoff, GRID_W), cs] = (out * _silu(zp)).astype(BF16)
        return carry

    lax.fori_loop(0, ROWS_PER_STEP, row_body, 0)


def _outproj_kernel(hf_ref, hb_ref, o_ref, z_ref, ya_ref, x_ref, mng_ref, w_ref, y_ref):
    hs = hf_ref[...].astype(F32) + hb_ref[...].astype(F32)
    og = _sigmoid(o_ref[...].astype(F32)) * hs
    mng = mng_ref[...]
    parts = []
    for h in range(H_M):
        cs = slice(h * DH_M, (h + 1) * DH_M)
        parts.append(_rms(og[:, cs], mng[:, cs]))
    ym = jnp.concatenate(parts, axis=1) * _silu(z_ref[...].astype(F32))
    mix = jnp.concatenate([ym.astype(BF16), ya_ref[...]], axis=1)
    y_ref[...] = x_ref[...] + _dot(mix, w_ref[...])


def _const(shape):
    nd = len(shape)
    return pl.BlockSpec(shape, lambda *_: (0,) * nd, pipeline_mode=pl.Buffered(1))


def _params(sem):
    return pltpu.CompilerParams(dimension_semantics=sem, vmem_limit_bytes=VMEM_LIMIT)


def _prep_weights(norm_g, w_in, b_gate, conv_w, mlstm_norm_g, q_norm_g, k_norm_g, rpb, w_out):
    wg = w_in[:, 5 * W_M:5 * W_M + N_GATE]
    w_all = jnp.concatenate(
        [w_in[:, :5 * W_M], jnp.pad(wg, ((0, 0), (0, LANES - N_GATE))), w_in[:, 5 * W_M + N_GATE:]],
        axis=1).astype(BF16)
    hh = jnp.arange(W_A) // DH_A
    c = jnp.arange(GRID_W)
    cstart = jnp.clip(c - WIN_W // 2, 0, GRID_W - WIN_W)
    allowed = (c[None, :] >= cstart[:, None]) & (c[None, :] < cstart[:, None] + WIN_W)
    col_idx = jnp.clip(c[None, :] - c[:, None] + (WIN_W - 1), 0, 2 * WIN_W - 2)
    row_idx = jnp.arange(WIN_H)[:, None] + jnp.arange(WIN_H)[None, :]
    b = rpb[:, row_idx[:, :, None, None], col_idx[None, None, :, :]]
    b = jnp.where(allowed[None, None, None], b, NEG_BIG)
    b = jnp.transpose(b, (0, 1, 3, 2, 4)).reshape(H_A // 2, 2, WIN_H, GRID_W, WIN_H * GRID_W)
    bias = jnp.transpose(b, (0, 2, 1, 3, 4)).reshape(H_A // 2, WIN_H, 2 * GRID_W, WIN_H * GRID_W)
    return dict(
        ng=norm_g.reshape(1, D_MODEL),
        w_all=w_all,
        wgt=wg.T.astype(BF16),
        bgr=jnp.pad(b_gate, (0, LANES - N_GATE)).reshape(1, LANES),
        bgc=jnp.broadcast_to(b_gate[:, None], (N_GATE, LANES)),
        cw=conv_w,
        qng=jnp.tile(q_norm_g, H_A).reshape(1, W_A),
        kng=jnp.tile(k_norm_g, H_A).reshape(1, W_A),
        bd=(hh[:, None] == hh[None, :]).astype(BF16),
        mng=mlstm_norm_g.reshape(1, W_M),
        bias=bias.astype(F32),
        w_out=w_out.astype(BF16),
    )


def _meta_proj(meta_tokens, p):
    out_shape = (
        jax.ShapeDtypeStruct((N_META, 2 * W_M), F32),
        jax.ShapeDtypeStruct((N_META, W_M), BF16),
        jax.ShapeDtypeStruct((N_META, LANES), F32),
        jax.ShapeDtypeStruct((N_META, W_A), BF16),
        jax.ShapeDtypeStruct((N_META, W_A), BF16),
    )
    return pl.pallas_call(_metaproj_kernel, out_shape=out_shape, name="meta_proj")(
        meta_tokens, p["ng"], p["w_all"], p["bgr"], p["kng"], p["bd"])


def _in_proj(x2d, tiles_per_seq, mpre, p):
    n = x2d.shape[0]
    nt = n // TM
    nseq = nt // tiles_per_seq
    hb = TM // HALO
    tile = lambda w: pl.BlockSpec((TM, w), lambda i: (i, 0))
    in_specs = [
        pl.BlockSpec((HALO, D_MODEL), lambda i: (jnp.maximum(i * hb - 1, 0), 0)),
        tile(D_MODEL),
        pl.BlockSpec((HALO, D_MODEL), lambda i: (jnp.minimum((i + 1) * hb, n // HALO - 1), 0)),
        _const((1, D_MODEL)), _const((D_MODEL, C_END)), _const((N_GATE, D_MODEL)),
        _const((N_META, 2 * W_M)), _const((1, LANES)), _const((N_GATE, LANES)),
        _const((CONV_K, 2 * W_M)), _const((1, W_A)), _const((1, W_A)), _const((W_A, W_A)),
    ]
    bf = lambda: jax.ShapeDtypeStruct((n, W_M), BF16)
    out_shape = (bf(), bf(), bf(), bf(), bf(),
                 jax.ShapeDtypeStruct((n, LANES), F32),
                 jax.ShapeDtypeStruct((N_GATE, n), F32),
                 bf(), bf(), bf(), bf(),
                 jax.ShapeDtypeStruct((nseq, N_META, W_M), F32))
    out_specs = (tile(W_M),) * 5 + (
        tile(LANES), pl.BlockSpec((N_GATE, TM), lambda i: (0, i)),
    ) + (tile(W_M),) * 4 + (
        pl.BlockSpec((1, N_META, W_M), lambda i: (i // tiles_per_seq, 0, 0)),)
    return pl.pallas_call(
        functools.partial(_inproj_kernel, tiles_per_seq),
        grid=(nt,), in_specs=in_specs, out_specs=out_specs, out_shape=out_shape,
        scratch_shapes=[pltpu.VMEM((TM + 2 * HALO, D_MODEL), BF16),
                        pltpu.VMEM((TM + 2 * HALO, 2 * W_M), F32),
                        pltpu.VMEM((8 + N_META + 8, W_M), F32)],
        compiler_params=_params(("arbitrary",)), name="in_proj",
    )(x2d, x2d, x2d, p["ng"], p["w_all"], p["wgt"], mpre, p["bgr"], p["bgc"],
      p["cw"], p["qng"], p["kng"], p["bd"])


def _mlstm(q, k, v, g, gt, mk, mv, mg, nseq, t_len):
    n = q.shape[0]
    nblk = t_len // LC
    fwd = lambda b, j: (b * nblk + j, 0)
    bwd = lambda b, j: (b * nblk + nblk - 1 - j, 0)
    fwd_t = lambda b, j: (0, b * nblk + j)
    bwd_t = lambda b, j: (0, b * nblk + nblk - 1 - j)

    def blocks(im, im_t):
        return [pl.BlockSpec((LC, W_M), im)] * 3 + [
            pl.BlockSpec((LC, LANES), im), pl.BlockSpec((N_GATE, LC), im_t)]

    in_specs = blocks(fwd, fwd_t) + blocks(bwd, bwd_t) + [
        pl.BlockSpec((1, N_META, W_M), lambda b, j: (b, 0, 0)),
        pl.BlockSpec((N_META, W_M), lambda b, j: (0, 0)),
        pl.BlockSpec((N_META, LANES), lambda b, j: (0, 0)),
    ]
    out = jax.ShapeDtypeStruct((n, W_M), BF16)
    return pl.pallas_call(
        _mlstm_kernel, grid=(nseq, nblk), in_specs=in_specs,
        out_specs=(pl.BlockSpec((LC, W_M), fwd), pl.BlockSpec((LC, W_M), bwd)),
        out_shape=(out, out),
        scratch_shapes=[pltpu.VMEM((2, H_M, DH_M, 2 * LANES), F32),
                        pltpu.VMEM((2, H_M, 8, LANES), F32)],
        compiler_params=_params(("arbitrary", "arbitrary")), name="mlstm",
    )(q, k, v, g, gt, q, k, v, g, gt, mk, mv, mg)


def _attention(qa, za, ka, va, mka, mva, bias, nseq, t_len):
    n = qa.shape[0]
    rows = t_len // GRID_W
    tq = ROWS_PER_STEP * GRID_W
    nb = t_len // tq
    qblk = pl.BlockSpec((tq, W_A), lambda b, j: (b * nb + j, 0))
    seq = pl.BlockSpec((t_len, W_A), lambda b, j: (b, 0))
    in_specs = [qblk, qblk, seq, seq, _const((N_META, W_A)), _const((N_META, W_A)),
                _const(bias.shape)]
    return pl.pallas_call(
        functools.partial(_attn_kernel, rows), grid=(nseq, nb), in_specs=in_specs,
        out_specs=qblk, out_shape=jax.ShapeDtypeStruct((n, W_A), BF16),
        compiler_params=_params(("arbitrary", "arbitrary")), name="attn",
    )(qa, za, ka, va, mka, mva, bias)


def _out_proj(hf, hb, o, z, ya, x2d, p):
    n = x2d.shape[0]
    tile = lambda w: pl.BlockSpec((TM, w), lambda i: (i, 0))
    in_specs = [tile(W_M)] * 5 + [tile(D_MODEL), _const((1, W_M)), _const((2 * W_M, D_MODEL))]
    return pl.pallas_call(
        _outproj_kernel, grid=(n // TM,), in_specs=in_specs, out_specs=tile(D_MODEL),
        out_shape=jax.ShapeDtypeStruct((n, D_MODEL), F32),
        compiler_params=_params(("arbitrary",)), name="out_proj",
    )(hf, hb, o, z, ya, x2d, p["mng"], p["w_out"])


def _trunk(x, meta, p):
    nseq, t_len, _ = x.shape
    assert t_len % TM == 0 and t_len % LC == 0 and t_len // GRID_W >= WIN_H
    mpre, mv, mg, mka, mva = meta
    x2d = x.reshape(nseq * t_len, D_MODEL)
    q, k, v, o, z, g, gt, qa, ka, va, za, mk = _in_proj(x2d, t_len // TM, mpre, p)
    hf, hb = _mlstm(q, k, v, g, gt, mk, mv, mg, nseq, t_len)
    ya = _attention(qa, za, ka, va, mka, mva, p["bias"], nseq, t_len)
    y = _out_proj(hf, hb, o, z, ya, x2d, p)
    return y.reshape(nseq, t_len, D_MODEL)


def kernel(x_prompt, x_sample, meta_tokens, norm_g, w_in, b_gate, conv_w, mlstm_norm_g,
           q_norm_g, k_norm_g, rpb, w_out):
    assert norm_g.shape[0] == 1, "single-layer trunk"
    p = _prep_weights(norm_g[0], w_in[0], b_gate[0], conv_w[0], mlstm_norm_g[0],
                      q_norm_g[0], k_norm_g[0], rpb[0], w_out[0])
    meta = _meta_proj(meta_tokens, p)
    return _trunk(x_prompt, meta, p), _trunk(x_sample, meta, p)
```

```python
import functools

import jax
import jax.numpy as jnp
from jax import lax
from jax.experimental import pallas as pl
from jax.experimental.pallas import tpu as pltpu

D_MODEL = 1024
N_META = 16
GRID_W = 64
W_M = 512
H_M = 4
DH_M = 128
W_A = 512
H_A = 8
DH_A = 64
WIN_H = 8
WIN_W = 16
CONV_K = 5
EPS = 1e-6
NEG_BIG = -1e30

C_QK, C_V, C_O, C_Z, C_G, C_QA, C_KA, C_VA, C_ZA, C_END = (
    0, 1024, 1536, 2048, 2560, 2688, 3200, 3712, 4224, 4736)
N_GATE = 4 * H_M
LANES = 128
TM = 512
HALO = 16
LC = 256
ROWS_PER_STEP = 8
VMEM_LIMIT = 56 * 1024 * 1024

F32 = jnp.float32
BF16 = jnp.bfloat16


def _dot(a, b):
    return jnp.dot(a, b, preferred_element_type=F32)


def _dot_nt(a, b):
    return lax.dot_general(a, b, (((1,), (1,)), ((), ())), preferred_element_type=F32)


def _dot_tn(a, b):
    return lax.dot_general(a, b, (((0,), (0,)), ((), ())), preferred_element_type=F32)


def _sigmoid(x):
    return 1.0 / (1.0 + jnp.exp(-x))


def _silu(x):
    return x * _sigmoid(x)


def _log_sigmoid(x):
    return jnp.minimum(x, 0.0) - jnp.log1p(jnp.exp(-jnp.abs(x)))


def _rms(x, g):
    return x * lax.rsqrt(jnp.mean(x * x, axis=-1, keepdims=True) + EPS) * g


def _gate_post(g, axis):
    ch = lax.broadcasted_iota(jnp.int32, g.shape, axis)
    is_f = ((ch >= 4) & (ch < 8)) | ((ch >= 12) & (ch < 16))
    return jnp.where(is_f, _log_sigmoid(g), g)


def _head_norm(t, bd, g, scale):
    ssq = _dot((t * t).astype(BF16), bd)
    return t * lax.rsqrt(ssq * (1.0 / DH_A) + EPS) * g * scale


def _conv_silu(xe, row0, rows, cw):
    n = xe.shape[0]
    acc = None
    for j in range(CONV_K):
        sh = row0 + j
        xs = xe if sh == 0 else pltpu.roll(xe, n - sh, 0)
        term = xs[0:rows] * cw[j:j + 1, :]
        acc = term if acc is None else acc + term
    return _silu(acc)


def _metaproj_kernel(mt_ref, ng_ref, w_ref, bgr_ref, kng_ref, bd_ref,
                     pre_ref, mv_ref, mg_ref, mka_ref, mva_ref):
    xn = _rms(mt_ref[...], ng_ref[...]).astype(BF16)
    pre_ref[...] = _dot(xn, w_ref[:, C_QK:C_V])
    mv_ref[...] = _dot(xn, w_ref[:, C_V:C_O]).astype(BF16)
    mg_ref[...] = _gate_post(_dot(xn, w_ref[:, C_G:C_QA]) + bgr_ref[...], 1)
    tk = _dot(xn, w_ref[:, C_KA:C_VA])
    mka_ref[...] = _head_norm(tk, bd_ref[...], kng_ref[...], 1.0).astype(BF16)
    mva_ref[...] = _dot(xn, w_ref[:, C_VA:C_ZA]).astype(BF16)


def _inproj_kernel(tiles_per_seq,
                   xl_ref, x_ref, xr_ref, ng_ref, w_ref, wgt_ref, mpre_ref, bgr_ref, bgc_ref,
                   cw_ref, qng_ref, kng_ref, bd_ref,
                   q_ref, k_ref, v_ref, o_ref, z_ref, g_ref, gt_ref,
                   qa_ref, ka_ref, va_ref, za_ref, mk_ref,
                   xn_scr, qk_scr):
    t = lax.rem(pl.program_id(0), tiles_per_seq)
    ng = ng_ref[...]
    xn_scr[0:HALO, :] = _rms(xl_ref[...], ng).astype(BF16)
    xn_scr[HALO:HALO + TM, :] = _rms(x_ref[...], ng).astype(BF16)
    xn_scr[HALO + TM:, :] = _rms(xr_ref[...], ng).astype(BF16)

    xe = xn_scr[...]
    qk_scr[:, 0:W_M] = _dot(xe, w_ref[:, 0:W_M])
    qk_scr[:, W_M:2 * W_M] = _dot(xe, w_ref[:, W_M:2 * W_M])

    mpre = mpre_ref[...]
    qk_scr[0:HALO, :] = jnp.where(t == 0, mpre, qk_scr[0:HALO, :])
    qk_scr[HALO + TM:, :] = jnp.where(t == tiles_per_seq - 1, 0.0, qk_scr[HALO + TM:, :])

    cw = cw_ref[...]
    q_ref[...] = _conv_silu(qk_scr[:, 0:W_M], HALO - 2, TM, cw[:, 0:W_M]).astype(BF16)
    kc = _conv_silu(qk_scr[:, W_M:], HALO - 2, TM, cw[:, W_M:])
    k_ref[...] = (kc * DH_M ** -0.5).astype(BF16)

    mcat = jnp.concatenate([jnp.zeros((8, W_M), F32), mpre[:, W_M:], qk_scr[HALO:HALO + 8, W_M:]], axis=0)
    mk_ref[0] = _conv_silu(mcat, 8 - 2, N_META, cw[:, W_M:]) * DH_M ** -0.5

    xm = xn_scr[HALO:HALO + TM, :]
    v_ref[...] = _dot(xm, w_ref[:, C_V:C_O]).astype(BF16)
    o_ref[...] = _dot(xm, w_ref[:, C_O:C_Z]).astype(BF16)
    z_ref[...] = _dot(xm, w_ref[:, C_Z:C_G]).astype(BF16)
    g_ref[...] = _gate_post(_dot(xm, w_ref[:, C_G:C_QA]) + bgr_ref[...], 1)
    gt_ref[...] = _gate_post(_dot_nt(wgt_ref[...], xm) + bgc_ref[:, 0:1], 0)
    bd = bd_ref[...]
    tq = _dot(xm, w_ref[:, C_QA:C_KA])
    qa_ref[...] = _head_norm(tq, bd, qng_ref[...], DH_A ** -0.5).astype(BF16)
    tk = _dot(xm, w_ref[:, C_KA:C_VA])
    ka_ref[...] = _head_norm(tk, bd, kng_ref[...], 1.0).astype(BF16)
    va_ref[...] = _dot(xm, w_ref[:, C_VA:C_ZA]).astype(BF16)
    za_ref[...] = _dot(xm, w_ref[:, C_ZA:C_END]).astype(BF16)


def _split2(x):
    hi = x.astype(BF16)
    return hi, (x - hi.astype(F32)).astype(BF16)


def _tri_cols(tri, g):
    hi, mid = _split2(g)
    r = _dot(tri, jnp.concatenate([hi, mid], axis=1))
    return r[:, :LANES] + r[:, LANES:]


def _tri_rows(gt, tri):
    hi, mid = _split2(gt)
    r = _dot(jnp.concatenate([hi, mid], axis=0), tri)
    return r[:N_GATE] + r[N_GATE:]


def _state_update(st_ref, m_ref, d, h, kh, vaug, bc, lic, tot, mprev, mst):
    wsc = tot - bc + lic
    mnew = jnp.maximum(tot + mprev, jnp.max(wsc, axis=0, keepdims=True))
    ws = jnp.exp(wsc - mnew)
    wsv = (ws * vaug.astype(F32)).astype(BF16)
    upd = _dot_tn(kh, wsv)
    if mst is not None:
        upd = jnp.exp(tot + mprev - mnew) * mst + upd
    st_ref[d, h] = upd
    m_ref[d, h] = jnp.broadcast_to(mnew, (8, LANES))


def _mlstm_kernel(qf_ref, kf_ref, vf_ref, gf_ref, gtf_ref,
                  qb_ref, kb_ref, vb_ref, gb_ref, gtb_ref,
                  mk_ref, mv_ref, mg_ref,
                  hf_ref, hb_ref, st_ref, m_ref):
    j = pl.program_id(1)
    ones_l = jnp.ones((LC, LANES), BF16)

    @pl.when(j == 0)
    def _():
        st_ref[1] = jnp.zeros((H_M, DH_M, 2 * LANES), F32)
        m_ref[1] = jnp.zeros((H_M, 8, LANES), F32)
        ri = lax.broadcasted_iota(jnp.int32, (N_META, N_META), 0)
        ci = lax.broadcasted_iota(jnp.int32, (N_META, N_META), 1)
        mg = mg_ref[...]
        bc_all = _tri_cols(jnp.where(ri >= ci, 1.0, 0.0).astype(BF16), mg)
        ones_m = jnp.ones((N_META, LANES), BF16)
        zero = jnp.zeros((1, 1), F32)
        for h in range(H_M):
            cs = slice(h * DH_M, (h + 1) * DH_M)
            bc = bc_all[:, H_M + h:H_M + h + 1]
            vaug = jnp.concatenate([mv_ref[:, cs], ones_m], axis=1)
            _state_update(st_ref, m_ref, 0, h, mk_ref[0, :, cs].astype(BF16), vaug,
                          bc, mg[:, h:h + 1], bc[N_META - 1:N_META], zero, None)

    ri = lax.broadcasted_iota(jnp.int32, (LC, LC), 0)
    ci = lax.broadcasted_iota(jnp.int32, (LC, LC), 1)
    low = ri >= ci
    upp = ri <= ci
    low_b = jnp.where(low, 1.0, 0.0).astype(BF16)
    upp_b = jnp.where(upp, 1.0, 0.0).astype(BF16)

    dirs = (
        (0, qf_ref, kf_ref, vf_ref, gf_ref, gtf_ref, hf_ref, low, low_b, upp_b, 0, LC - 1),
        (1, qb_ref, kb_ref, vb_ref, gb_ref, gtb_ref, hb_ref, upp, upp_b, low_b, 2 * H_M, 0),
    )
    for d, q_ref, k_ref, v_ref, g_ref, gt_ref, h_ref, mask, tri_c, tri_r, ch0, last in dirs:
        g = g_ref[...]
        gt = gt_ref[...]
        bc_all = _tri_cols(tri_c, g)
        br_all = _tri_rows(gt, tri_r)
        for h in range(H_M):
            cs = slice(h * DH_M, (h + 1) * DH_M)
            ci_, cf_ = ch0 + h, ch0 + H_M + h
            qh, kh, vh = q_ref[:, cs], k_ref[:, cs], v_ref[:, cs]
            bc = bc_all[:, cf_:cf_ + 1]
            lic = g[:, ci_:ci_ + 1]
            ur = gt[ci_:ci_ + 1, :] - br_all[cf_:cf_ + 1, :]
            dm = jnp.where(mask, bc + ur, -jnp.inf)
            mprev = m_ref[d, h][0:1, 0:1]
            gc = bc + mprev
            mj = jnp.maximum(gc, jnp.max(dm, axis=1, keepdims=True))
            s = (_dot_nt(qh, kh) * jnp.exp(dm - mj)).astype(BF16)
            vaug = jnp.concatenate([vh, ones_l], axis=1)
            mst = st_ref[d, h]
            tt = _dot(s, vaug) + jnp.exp(gc - mj) * _dot(qh, mst.astype(BF16))
            hout = tt[:, :LANES] / jnp.maximum(jnp.abs(tt[:, LANES:]), jnp.exp(-mj))
            h_ref[:, cs] = hout.astype(h_ref.dtype)
            _state_update(st_ref, m_ref, d, h, kh, vaug, bc, lic, bc[last:last + 1], mprev, mst)


def _attn_kernel(rows, q_ref, z_ref, k_ref, v_ref, mk_ref, mv_ref, bias_ref, y_ref):
    jb = pl.program_id(1)
    lo = lax.broadcasted_iota(jnp.int32, (GRID_W, LANES), 1) < DH_A
    ones_w = jnp.ones((WIN_H * GRID_W, LANES), BF16)
    ones_m = jnp.ones((N_META, LANES), BF16)

    for rr in range(ROWS_PER_STEP):
        r = jb * ROWS_PER_STEP + rr
        rs = jnp.clip(r - WIN_H // 2, 0, rows - WIN_H)
        delta = rs - r + (WIN_H - 1)
        koff = pl.multiple_of(rs * GRID_W, GRID_W)
        qoff = rr * GRID_W
        for p in range(H_A // 2):
            cs = slice(p * LANES, (p + 1) * LANES)
            qp = q_ref[pl.ds(qoff, GRID_W), cs]
            zero = jnp.zeros_like(qp)
            qs = jnp.concatenate([jnp.where(lo, qp, zero), jnp.where(lo, zero, qp)], axis=0)
            s = _dot_nt(qs, k_ref[pl.ds(koff, WIN_H * GRID_W), cs]) + bias_ref[p, delta]
            sm = _dot_nt(qs, mk_ref[:, cs])
            mx = jnp.maximum(jnp.max(s, axis=1, keepdims=True), jnp.max(sm, axis=1, keepdims=True))
            e = jnp.exp(s - mx).astype(BF16)
            em = jnp.exp(sm - mx).astype(BF16)
            vaug = jnp.concatenate([v_ref[pl.ds(koff, WIN_H * GRID_W), cs], ones_w], axis=1)
            mvaug = jnp.concatenate([mv_ref[:, cs], ones_m], axis=1)
            o = _dot(e, vaug) + _dot(em, mvaug)
            on = o[:, :LANES] / o[:, LANES:]
            out = jnp.where(lo, on[0:GRID_W], on[GRID_W:])
            zp = z_ref[pl.ds(qoff, GRID_W), cs].astype(F32)
            y_ref[pl.ds(qoff, GRID_W), cs] = (out * _silu(zp)).astype(BF16)


def _outproj_kernel(hf_ref, hb_ref, o_ref, z_ref, ya_ref, x_ref, mng_ref, w_ref, y_ref):
    hs = hf_ref[...].astype(F32) + hb_ref[...].astype(F32)
    og = _sigmoid(o_ref[...].astype(F32)) * hs
    mng = mng_ref[...]
    parts = []
    for h in range(H_M):
        cs = slice(h * DH_M, (h + 1) * DH_M)
        parts.append(_rms(og[:, cs], mng[:, cs]))
    ym = jnp.concatenate(parts, axis=1) * _silu(z_ref[...].astype(F32))
    mix = jnp.concatenate([ym.astype(BF16), ya_ref[...]], axis=1)
    y_ref[...] = x_ref[...] + _dot(mix, w_ref[...])


def _const(shape):
    nd = len(shape)
    return pl.BlockSpec(shape, lambda *_: (0,) * nd, pipeline_mode=pl.Buffered(1))


def _params(sem):
    return pltpu.CompilerParams(dimension_semantics=sem, vmem_limit_bytes=VMEM_LIMIT)


def _rel_bias(rpb):
    n_row = 2 * WIN_H - 1
    period = 2 * GRID_W + 1
    seq = jnp.concatenate(
        [rpb[..., WIN_W - 1:], jnp.zeros((H_A, n_row, period - (2 * WIN_W - 1)), rpb.dtype),
         rpb[..., :WIN_W - 1]], axis=-1)
    toe = jnp.tile(seq, (1, 1, GRID_W))[..., :GRID_W * 2 * GRID_W]
    toe = toe.reshape(H_A, n_row, GRID_W, 2 * GRID_W)[..., :GRID_W]
    c = jnp.arange(GRID_W)
    cstart = jnp.clip(c - WIN_W // 2, 0, GRID_W - WIN_W)
    allowed = (c[None, :] >= cstart[:, None]) & (c[None, :] < cstart[:, None] + WIN_W)
    toe = jnp.where(allowed[None, None], toe, NEG_BIG)
    b = jnp.stack([toe[:, d:d + WIN_H] for d in range(WIN_H)], axis=1)
    b = jnp.transpose(b, (0, 1, 3, 2, 4)).reshape(H_A // 2, 2, WIN_H, GRID_W, WIN_H * GRID_W)
    return jnp.transpose(b, (0, 2, 1, 3, 4)).reshape(H_A // 2, WIN_H, 2 * GRID_W, WIN_H * GRID_W)


def _prep_weights(norm_g, w_in, b_gate, conv_w, mlstm_norm_g, q_norm_g, k_norm_g, rpb, w_out):
    wg = w_in[:, 5 * W_M:5 * W_M + N_GATE]
    w_all = jnp.concatenate(
        [w_in[:, :5 * W_M], jnp.pad(wg, ((0, 0), (0, LANES - N_GATE))), w_in[:, 5 * W_M + N_GATE:]],
        axis=1).astype(BF16)
    hh = jnp.arange(W_A) // DH_A
    return dict(
        ng=norm_g.reshape(1, D_MODEL),
        w_all=w_all,
        wgt=wg.T.astype(BF16),
        bgr=jnp.pad(b_gate, (0, LANES - N_GATE)).reshape(1, LANES),
        bgc=jnp.broadcast_to(b_gate[:, None], (N_GATE, LANES)),
        cw=conv_w,
        qng=jnp.tile(q_norm_g, H_A).reshape(1, W_A),
        kng=jnp.tile(k_norm_g, H_A).reshape(1, W_A),
        bd=(hh[:, None] == hh[None, :]).astype(BF16),
        mng=mlstm_norm_g.reshape(1, W_M),
        bias=_rel_bias(rpb),
        w_out=w_out.astype(BF16),
    )


def _meta_proj(meta_tokens, p):
    out_shape = (
        jax.ShapeDtypeStruct((N_META, 2 * W_M), F32),
        jax.ShapeDtypeStruct((N_META, W_M), BF16),
        jax.ShapeDtypeStruct((N_META, LANES), F32),
        jax.ShapeDtypeStruct((N_META, W_A), BF16),
        jax.ShapeDtypeStruct((N_META, W_A), BF16),
    )
    return pl.pallas_call(_metaproj_kernel, out_shape=out_shape, name="meta_proj")(
        meta_tokens, p["ng"], p["w_all"], p["bgr"], p["kng"], p["bd"])


def _in_proj(x2d, tiles_per_seq, mpre, p):
    n = x2d.shape[0]
    nt = n // TM
    nseq = nt // tiles_per_seq
    hb = TM // HALO
    tile = lambda w: pl.BlockSpec((TM, w), lambda i: (i, 0))
    in_specs = [
        pl.BlockSpec((HALO, D_MODEL), lambda i: (jnp.maximum(i * hb - 1, 0), 0)),
        tile(D_MODEL),
        pl.BlockSpec((HALO, D_MODEL), lambda i: (jnp.minimum((i + 1) * hb, n // HALO - 1), 0)),
        _const((1, D_MODEL)), _const((D_MODEL, C_END)), _const((N_GATE, D_MODEL)),
        _const((N_META, 2 * W_M)), _const((1, LANES)), _const((N_GATE, LANES)),
        _const((CONV_K, 2 * W_M)), _const((1, W_A)), _const((1, W_A)), _const((W_A, W_A)),
    ]
    bf = lambda: jax.ShapeDtypeStruct((n, W_M), BF16)
    out_shape = (bf(), bf(), bf(), bf(), bf(),
                 jax.ShapeDtypeStruct((n, LANES), F32),
                 jax.ShapeDtypeStruct((N_GATE, n), F32),
                 bf(), bf(), bf(), bf(),
                 jax.ShapeDtypeStruct((nt, N_META, W_M), F32))
    out_specs = (tile(W_M),) * 5 + (
        tile(LANES), pl.BlockSpec((N_GATE, TM), lambda i: (0, i)),
    ) + (tile(W_M),) * 4 + (
        pl.BlockSpec((1, N_META, W_M), lambda i: (i, 0, 0)),)
    return pl.pallas_call(
        functools.partial(_inproj_kernel, tiles_per_seq),
        grid=(nt,), in_specs=in_specs, out_specs=out_specs, out_shape=out_shape,
        scratch_shapes=[pltpu.VMEM((TM + 2 * HALO, D_MODEL), BF16),
                        pltpu.VMEM((TM + 2 * HALO, 2 * W_M), F32)],
        compiler_params=_params(("arbitrary",)), name="in_proj",
    )(x2d, x2d, x2d, p["ng"], p["w_all"], p["wgt"], mpre, p["bgr"], p["bgc"],
      p["cw"], p["qng"], p["kng"], p["bd"])


def _mlstm(q, k, v, g, gt, mk, mv, mg, nseq, t_len):
    n = q.shape[0]
    nblk = t_len // LC
    fwd = lambda b, j: (b * nblk + j, 0)
    bwd = lambda b, j: (b * nblk + nblk - 1 - j, 0)
    fwd_t = lambda b, j: (0, b * nblk + j)
    bwd_t = lambda b, j: (0, b * nblk + nblk - 1 - j)

    def blocks(im, im_t):
        return [pl.BlockSpec((LC, W_M), im)] * 3 + [
            pl.BlockSpec((LC, LANES), im), pl.BlockSpec((N_GATE, LC), im_t)]

    in_specs = blocks(fwd, fwd_t) + blocks(bwd, bwd_t) + [
        pl.BlockSpec((1, N_META, W_M), lambda b, j: (b * (t_len // TM), 0, 0)),
        pl.BlockSpec((N_META, W_M), lambda b, j: (0, 0)),
        pl.BlockSpec((N_META, LANES), lambda b, j: (0, 0)),
    ]
    out = jax.ShapeDtypeStruct((n, W_M), BF16)
    return pl.pallas_call(
        _mlstm_kernel, grid=(nseq, nblk), in_specs=in_specs,
        out_specs=(pl.BlockSpec((LC, W_M), fwd), pl.BlockSpec((LC, W_M), bwd)),
        out_shape=(out, out),
        scratch_shapes=[pltpu.VMEM((2, H_M, DH_M, 2 * LANES), F32),
                        pltpu.VMEM((2, H_M, 8, LANES), F32)],
        compiler_params=_params(("arbitrary", "arbitrary")), name="mlstm",
    )(q, k, v, g, gt, q, k, v, g, gt, mk, mv, mg)


def _attention(qa, za, ka, va, mka, mva, bias, nseq, t_len):
    n = qa.shape[0]
    rows = t_len // GRID_W
    tq = ROWS_PER_STEP * GRID_W
    nb = t_len // tq
    qblk = pl.BlockSpec((tq, W_A), lambda b, j: (b * nb + j, 0))
    seq = pl.BlockSpec((t_len, W_A), lambda b, j: (b, 0))
    in_specs = [qblk, qblk, seq, seq, _const((N_META, W_A)), _const((N_META, W_A)),
                _const(bias.shape)]
    return pl.pallas_call(
        functools.partial(_attn_kernel, rows), grid=(nseq, nb), in_specs=in_specs,
        out_specs=qblk, out_shape=jax.ShapeDtypeStruct((n, W_A), BF16),
        compiler_params=_params(("arbitrary", "arbitrary")), name="attn",
    )(qa, za, ka, va, mka, mva, bias)


def _out_proj(hf, hb, o, z, ya, x2d, p):
    n = x2d.shape[0]
    tile = lambda w: pl.BlockSpec((TM, w), lambda i: (i, 0))
    in_specs = [tile(W_M)] * 5 + [tile(D_MODEL), _const((1, W_M)), _const((2 * W_M, D_MODEL))]
    return pl.pallas_call(
        _outproj_kernel, grid=(n // TM,), in_specs=in_specs, out_specs=tile(D_MODEL),
        out_shape=jax.ShapeDtypeStruct((n, D_MODEL), F32),
        compiler_params=_params(("arbitrary",)), name="out_proj",
    )(hf, hb, o, z, ya, x2d, p["mng"], p["w_out"])


def _trunk(x, meta, p):
    nseq, t_len, _ = x.shape
    assert t_len % TM == 0 and t_len % LC == 0 and t_len // GRID_W >= WIN_H
    mpre, mv, mg, mka, mva = meta
    x2d = x.reshape(nseq * t_len, D_MODEL)
    q, k, v, o, z, g, gt, qa, ka, va, za, mk = _in_proj(x2d, t_len // TM, mpre, p)
    hf, hb = _mlstm(q, k, v, g, gt, mk, mv, mg, nseq, t_len)
    ya = _attention(qa, za, ka, va, mka, mva, p["bias"], nseq, t_len)
    y = _out_proj(hf, hb, o, z, ya, x2d, p)
    return y.reshape(nseq, t_len, D_MODEL)


def kernel(x_prompt, x_sample, meta_tokens, norm_g, w_in, b_gate, conv_w, mlstm_norm_g,
           q_norm_g, k_norm_g, rpb, w_out):
    assert norm_g.shape[0] == 1, "single-layer trunk"
    p = _prep_weights(norm_g[0], w_in[0], b_gate[0], conv_w[0], mlstm_norm_g[0],
                      q_norm_g[0], k_norm_g[0], rpb[0], w_out[0])
    meta = _meta_proj(meta_tokens, p)
    return _trunk(x_prompt, meta, p), _trunk(x_sample, meta, p)
```

```python
import functools

import jax
import jax.numpy as jnp
from jax import lax
from jax.experimental import pallas as pl
from jax.experimental.pallas import tpu as pltpu

D_MODEL = 1024
N_META = 16
GRID_W = 64
W_M = 512
H_M = 4
DH_M = 128
W_A = 512
H_A = 8
DH_A = 64
WIN_H = 8
WIN_W = 16
CONV_K = 5
EPS = 1e-6
NEG_BIG = -1e30
LOG2E = 1.4426950408889634

LANES = 128
C_QK, C_V, C_O, C_Z, C_G, C_QA, C_KA, C_VA, C_ZA, C_END = (
    0, 1024, 1536, 2048, 2560, 2816, 3328, 3840, 4352, 4864)
N_CH = 2 * H_M
TM = 512
HALO = 16
LC = 256
ROWS_PER_STEP = TM // GRID_W
VMEM_LIMIT = 60 * 1024 * 1024
KV_DOUBLE_BUFFER_BYTES = 16 * 1024 * 1024

F32 = jnp.float32
BF16 = jnp.bfloat16


def _dot(a, b):
    return jnp.dot(a, b, preferred_element_type=F32)


def _dot_nt(a, b):
    return lax.dot_general(a, b, (((1,), (1,)), ((), ())), preferred_element_type=F32)


def _dot_tn(a, b):
    return lax.dot_general(a, b, (((0,), (0,)), ((), ())), preferred_element_type=F32)


def _sigmoid(x):
    return 1.0 / (1.0 + jnp.exp(-x))


def _silu(x):
    return x * _sigmoid(x)


def _log_sigmoid(x):
    return jnp.minimum(x, 0.0) - jnp.log1p(jnp.exp(-jnp.abs(x)))


def _rms(x, g):
    return x * lax.rsqrt(jnp.mean(x * x, axis=-1, keepdims=True) + EPS) * g


def _gates_cols(g):
    return jnp.concatenate([g[:, :LANES], _log_sigmoid(g[:, LANES:])], axis=1) * LOG2E


def _head_norm(t, bd, g, scale):
    ssq = _dot((t * t).astype(BF16), bd)
    return t * lax.rsqrt(ssq * (1.0 / DH_A) + EPS) * g * scale


def _conv_silu(xe, row0, rows, cw):
    n = xe.shape[0]
    acc = None
    for j in range(CONV_K):
        sh = row0 + j
        xs = xe if sh == 0 else pltpu.roll(xe, n - sh, 0)
        term = xs[0:rows] * cw[j:j + 1, :]
        acc = term if acc is None else acc + term
    return _silu(acc)


def _metaproj_kernel(mt_ref, ng_ref, w_ref, bgr_ref, kng_ref, bd_ref,
                     pre_ref, mv_ref, mg_ref, mka_ref, mva_ref):
    xn = _rms(mt_ref[...], ng_ref[...]).astype(BF16)
    pre_ref[...] = _dot(xn, w_ref[:, C_QK:C_V])
    mv_ref[...] = _dot(xn, w_ref[:, C_V:C_O]).astype(BF16)
    mg_ref[...] = _gates_cols(_dot(xn, w_ref[:, C_G:C_QA]) + bgr_ref[...])
    tk = _dot(xn, w_ref[:, C_KA:C_VA])
    mka_ref[...] = _head_norm(tk, bd_ref[...], kng_ref[...], 1.0).astype(BF16)
    mva_ref[...] = _dot(xn, w_ref[:, C_VA:C_ZA]).astype(BF16)


def _inproj_kernel(tiles_per_seq,
                   xl_ref, x_ref, xr_ref, ng_ref, w_ref, wgt_ref, mpre_ref, bgr_ref, bgc_ref,
                   cw_ref, qng_ref, kng_ref, bd_ref,
                   q_ref, k_ref, v_ref, o_ref, z_ref, g_ref, gt_ref,
                   qa_ref, ka_ref, va_ref, za_ref, mk_ref,
                   xn_scr, qk_scr):
    t = lax.rem(pl.program_id(0), tiles_per_seq)
    ng = ng_ref[...]
    xn_scr[0:HALO, :] = _rms(xl_ref[...], ng).astype(BF16)
    xn_scr[HALO:HALO + TM, :] = _rms(x_ref[...], ng).astype(BF16)
    xn_scr[HALO + TM:, :] = _rms(xr_ref[...], ng).astype(BF16)

    xe = xn_scr[...]
    qk_scr[:, 0:W_M] = _dot(xe, w_ref[:, 0:W_M])
    qk_scr[:, W_M:2 * W_M] = _dot(xe, w_ref[:, W_M:2 * W_M])

    mpre = mpre_ref[...]
    qk_scr[0:HALO, :] = jnp.where(t == 0, mpre, qk_scr[0:HALO, :])
    qk_scr[HALO + TM:, :] = jnp.where(t == tiles_per_seq - 1, 0.0, qk_scr[HALO + TM:, :])

    cw = cw_ref[...]
    xm = lambda: xn_scr[HALO:HALO + TM, :]
    bd = bd_ref[...]

    q_ref[...] = _conv_silu(qk_scr[:, 0:W_M], HALO - 2, TM, cw[:, 0:W_M]).astype(BF16)
    kc = _conv_silu(qk_scr[:, W_M:], HALO - 2, TM, cw[:, W_M:])
    k_ref[...] = (kc * DH_M ** -0.5).astype(BF16)

    v_ref[...] = _dot(xm(),w_ref[:, C_V:C_O]).astype(BF16)
    o_ref[...] = _dot(xm(),w_ref[:, C_O:C_Z]).astype(BF16)
    z_ref[...] = _dot(xm(),w_ref[:, C_Z:C_G]).astype(BF16)
    g_ref[...] = _gates_cols(_dot(xm(),w_ref[:, C_G:C_QA]) + bgr_ref[...])
    gt = _dot_nt(wgt_ref[...], xm()) + bgc_ref[:, 0:1]
    is_f = lax.broadcasted_iota(jnp.int32, gt.shape, 0) >= N_CH
    gt_ref[...] = jnp.where(is_f, _log_sigmoid(gt), gt) * LOG2E
    tq = _dot(xm(),w_ref[:, C_QA:C_KA])
    qa_ref[...] = _head_norm(tq, bd, qng_ref[...], DH_A ** -0.5).astype(BF16)
    tk = _dot(xm(),w_ref[:, C_KA:C_VA])
    ka_ref[...] = _head_norm(tk, bd, kng_ref[...], 1.0).astype(BF16)
    va_ref[...] = _dot(xm(),w_ref[:, C_VA:C_ZA]).astype(BF16)
    za_ref[...] = _dot(xm(),w_ref[:, C_ZA:C_END]).astype(BF16)

    mcat = jnp.concatenate([jnp.zeros((8, W_M), F32), mpre[:, W_M:], qk_scr[HALO:HALO + 8, W_M:]], axis=0)
    mk_ref[0] = _conv_silu(mcat, 8 - 2, N_META, cw[:, W_M:]) * DH_M ** -0.5


def _split2(x):
    hi = x.astype(BF16)
    return hi, (x - hi.astype(F32)).astype(BF16)


def _tri_cols(tri, g):
    hi, mid = _split2(g)
    r = _dot(tri, jnp.concatenate([hi, mid], axis=1))
    return r[:, :LANES] + r[:, LANES:]


def _tri_rows(gt, tri):
    hi, mid = _split2(gt)
    r = _dot(jnp.concatenate([hi, mid], axis=0), tri)
    return r[:gt.shape[0]] + r[gt.shape[0]:]


def _mlstm_kernel(qf_ref, kf_ref, vf_ref, gf_ref, gtf_ref,
                  qb_ref, kb_ref, vb_ref, gb_ref, gtb_ref,
                  mk_ref, mv_ref, mg_ref,
                  hf_ref, hb_ref, st_ref, m_ref):
    j = pl.program_id(1)
    ones_l = jnp.ones((LC, LANES), BF16)
    lane = lax.broadcasted_iota(jnp.int32, (1, LANES), 1)

    @pl.when(j == 0)
    def _():
        st_ref[1] = jnp.zeros((H_M, DH_M, 2 * LANES), F32)
        m_ref[1] = jnp.zeros((8, LANES), F32)
        ri = lax.broadcasted_iota(jnp.int32, (N_META, N_META), 0)
        ci = lax.broadcasted_iota(jnp.int32, (N_META, N_META), 1)
        mg = mg_ref[...]
        li, lf = mg[:, :LANES], mg[:, LANES:]
        bc = _tri_cols(jnp.where(ri >= ci, 1.0, 0.0).astype(BF16), lf)
        tot = bc[N_META - 1:N_META]
        wsc = tot - bc + li
        mnew = jnp.maximum(tot, jnp.max(wsc, axis=0, keepdims=True))
        ws = jnp.exp2(wsc - mnew)
        ones_m = jnp.ones((N_META, LANES), BF16)
        for h in range(H_M):
            cs = slice(h * DH_M, (h + 1) * DH_M)
            wk = (mk_ref[0, :, cs] * ws[:, h:h + 1]).astype(BF16)
            st_ref[0, h] = _dot_tn(wk, jnp.concatenate([mv_ref[:, cs], ones_m], axis=1))
        m_ref[0] = jnp.broadcast_to(mnew, (8, LANES))

    ri = lax.broadcasted_iota(jnp.int32, (LC, LC), 0)
    ci = lax.broadcasted_iota(jnp.int32, (LC, LC), 1)
    low = ri >= ci
    upp = ri <= ci
    low_b = jnp.where(low, 1.0, 0.0).astype(BF16)
    upp_b = jnp.where(upp, 1.0, 0.0).astype(BF16)

    dirs = (
        (qf_ref, kf_ref, vf_ref, gf_ref, gtf_ref, hf_ref, low, low_b, upp_b, LC - 1),
        (qb_ref, kb_ref, vb_ref, gb_ref, gtb_ref, hb_ref, upp, upp_b, low_b, 0),
    )
    for d, (q_ref, k_ref, v_ref, g_ref, gt_ref, h_ref, mask, tri_c, tri_r, last) in enumerate(dirs):
        g = g_ref[...]
        li, lf = g[:, :LANES], g[:, LANES:]
        gt = gt_ref[...]
        bc = _tri_cols(tri_c, lf)
        br = _tri_rows(gt, tri_r)[N_CH:]
        ur = gt[:N_CH] - br
        mprev = m_ref[d][0:1]
        tot = bc[last:last + 1]
        wsc = tot - bc + li
        mnew = jnp.maximum(tot + mprev, jnp.max(wsc, axis=0, keepdims=True))
        decay = jnp.exp2(tot + mprev - mnew)
        ws = jnp.exp2(wsc - mnew)
        m_ref[d] = jnp.broadcast_to(mnew, (8, LANES))
        for h in range(H_M):
            c = H_M * d + h
            cc = slice(c, c + 1)
            cs = slice(h * DH_M, (h + 1) * DH_M)
            qh, kh, vh = q_ref[:, cs], k_ref[:, cs], v_ref[:, cs]
            dm = jnp.where(mask, ur[c:c + 1], -jnp.inf)
            mp = jnp.max(jnp.where(lane == c, mprev, -jnp.inf), axis=1, keepdims=True)
            a = jnp.maximum(mp, jnp.max(dm, axis=1, keepdims=True))
            s = (_dot_nt(qh, kh) * jnp.exp2(dm - a)).astype(BF16)
            wq = (qh.astype(F32) * jnp.exp2(mp - a)).astype(BF16)
            vaug = jnp.concatenate([vh, ones_l], axis=1)
            mst = st_ref[d, h]
            tt = _dot(jnp.concatenate([s, wq], axis=1),
                      jnp.concatenate([vaug, mst.astype(BF16)], axis=0))
            e_inv = jnp.exp2(-(bc[:, cc] + a))
            hout = tt[:, :LANES] / jnp.maximum(jnp.abs(tt[:, LANES:]), e_inv)
            h_ref[:, cs] = hout.astype(h_ref.dtype)
            wk = (kh.astype(F32) * ws[:, cc]).astype(BF16)
            st_ref[d, h] = decay[:, cc] * mst + _dot_tn(wk, vaug)


def _attn_out_kernel(rows, q_ref, z_ref, k_ref, v_ref, mk_ref, mv_ref, bias_ref,
                     hf_ref, hb_ref, o_ref, zm_ref, x_ref, mng_ref, w_ref,
                     y_ref, mix_scr):
    jb = pl.program_id(1)
    lo = lax.broadcasted_iota(jnp.int32, (GRID_W, LANES), 1) < DH_A
    ones_w = jnp.ones((WIN_H * GRID_W, LANES), BF16)
    ones_m = jnp.ones((N_META, LANES), BF16)
    mng = mng_ref[...]
    half = ROWS_PER_STEP // 2

    for rr in range(ROWS_PER_STEP):
        r = jb * ROWS_PER_STEP + rr
        rs = jnp.clip(r - WIN_H // 2, 0, rows - WIN_H)
        delta = rs - r + (WIN_H - 1)
        koff = pl.multiple_of(rs * GRID_W, GRID_W)
        tok = slice(rr * GRID_W, (rr + 1) * GRID_W)
        for p in range(H_A // 2):
            cs = slice(p * LANES, (p + 1) * LANES)
            qp = q_ref[tok, cs]
            zero = jnp.zeros_like(qp)
            qs = jnp.concatenate([jnp.where(lo, qp, zero), jnp.where(lo, zero, qp)], axis=0)
            s = _dot_nt(qs, k_ref[pl.ds(koff, WIN_H * GRID_W), cs]) + bias_ref[p, delta]
            sm = _dot_nt(qs, mk_ref[:, cs])
            mx = jnp.maximum(jnp.max(s, axis=1, keepdims=True), jnp.max(sm, axis=1, keepdims=True))
            e = jnp.exp(s - mx).astype(BF16)
            em = jnp.exp(sm - mx).astype(BF16)
            vaug = jnp.concatenate([v_ref[pl.ds(koff, WIN_H * GRID_W), cs], ones_w], axis=1)
            mvaug = jnp.concatenate([mv_ref[:, cs], ones_m], axis=1)
            o = _dot(e, vaug) + _dot(em, mvaug)
            on = o[:, :LANES] / o[:, LANES:]
            out = jnp.where(lo, on[0:GRID_W], on[GRID_W:])
            zp = z_ref[tok, cs].astype(F32)
            mix_scr[tok, W_M + p * LANES:W_M + (p + 1) * LANES] = (out * _silu(zp)).astype(BF16)

        hs = hf_ref[tok, :].astype(F32) + hb_ref[tok, :].astype(F32)
        og = _sigmoid(o_ref[tok, :].astype(F32)) * hs
        ym = jnp.concatenate(
            [_rms(og[:, h * DH_M:(h + 1) * DH_M], mng[:, h * DH_M:(h + 1) * DH_M]) for h in range(H_M)],
            axis=1) * _silu(zm_ref[tok, :].astype(F32))
        mix_scr[tok, 0:W_M] = ym.astype(BF16)

        if rr % half == half - 1:
            rows_h = slice((rr + 1 - half) * GRID_W, (rr + 1) * GRID_W)
            y_ref[rows_h, :] = x_ref[rows_h, :] + _dot(mix_scr[rows_h, :], w_ref[...])


def _const(shape):
    nd = len(shape)
    return pl.BlockSpec(shape, lambda *_: (0,) * nd, pipeline_mode=pl.Buffered(1))


def _params(sem):
    return pltpu.CompilerParams(dimension_semantics=sem, vmem_limit_bytes=VMEM_LIMIT)


def _rel_bias(rpb):
    n_row = 2 * WIN_H - 1
    period = 2 * GRID_W + 1
    seq = jnp.concatenate(
        [rpb[..., WIN_W - 1:], jnp.zeros((H_A, n_row, period - (2 * WIN_W - 1)), rpb.dtype),
         rpb[..., :WIN_W - 1]], axis=-1)
    toe = jnp.tile(seq, (1, 1, GRID_W))[..., :GRID_W * 2 * GRID_W]
    toe = toe.reshape(H_A, n_row, GRID_W, 2 * GRID_W)[..., :GRID_W]
    c = jnp.arange(GRID_W)
    cstart = jnp.clip(c - WIN_W // 2, 0, GRID_W - WIN_W)
    allowed = (c[None, :] >= cstart[:, None]) & (c[None, :] < cstart[:, None] + WIN_W)
    toe = jnp.where(allowed[None, None], toe, NEG_BIG)
    b = jnp.stack([toe[:, d:d + WIN_H] for d in range(WIN_H)], axis=1)
    b = jnp.transpose(b, (0, 1, 3, 2, 4)).reshape(H_A // 2, 2, WIN_H, GRID_W, WIN_H * GRID_W)
    return jnp.transpose(b, (0, 2, 1, 3, 4)).reshape(H_A // 2, WIN_H, 2 * GRID_W, WIN_H * GRID_W)


def _gate_channels(a):
    i_f, f_f, i_b, f_b = (a[..., k * H_M:(k + 1) * H_M] for k in range(4))
    return jnp.concatenate([i_f, i_b], axis=-1), jnp.concatenate([f_f, f_b], axis=-1)


def _prep_weights(norm_g, w_in, b_gate, conv_w, mlstm_norm_g, q_norm_g, k_norm_g, rpb, w_out):
    g0 = 5 * W_M
    w_i, w_f = _gate_channels(w_in[:, g0:g0 + 4 * H_M])
    b_i, b_f = _gate_channels(b_gate)
    padl = lambda a: jnp.pad(a, [(0, 0)] * (a.ndim - 1) + [(0, LANES - N_CH)])
    w_all = jnp.concatenate(
        [w_in[:, :g0], padl(w_i), padl(w_f), w_in[:, g0 + 4 * H_M:]], axis=1).astype(BF16)
    hh = jnp.arange(W_A) // DH_A
    return dict(
        ng=norm_g.reshape(1, D_MODEL),
        w_all=w_all,
        wgt=jnp.concatenate([w_i, w_f], axis=1).T.astype(BF16),
        bgr=jnp.concatenate([padl(b_i), padl(b_f)]).reshape(1, 2 * LANES),
        bgc=jnp.broadcast_to(jnp.concatenate([b_i, b_f])[:, None], (2 * N_CH, LANES)),
        cw=conv_w,
        qng=jnp.tile(q_norm_g, H_A).reshape(1, W_A),
        kng=jnp.tile(k_norm_g, H_A).reshape(1, W_A),
        bd=(hh[:, None] == hh[None, :]).astype(BF16),
        mng=mlstm_norm_g.reshape(1, W_M),
        bias=_rel_bias(rpb),
        w_out=w_out.astype(BF16),
    )


def _meta_proj(meta_tokens, p):
    out_shape = (
        jax.ShapeDtypeStruct((N_META, 2 * W_M), F32),
        jax.ShapeDtypeStruct((N_META, W_M), BF16),
        jax.ShapeDtypeStruct((N_META, 2 * LANES), F32),
        jax.ShapeDtypeStruct((N_META, W_A), BF16),
        jax.ShapeDtypeStruct((N_META, W_A), BF16),
    )
    return pl.pallas_call(_metaproj_kernel, out_shape=out_shape, name="meta_proj")(
        meta_tokens, p["ng"], p["w_all"], p["bgr"], p["kng"], p["bd"])


def _in_proj(x2d, tiles_per_seq, mpre, p):
    n = x2d.shape[0]
    nt = n // TM
    hb = TM // HALO
    tile = lambda w: pl.BlockSpec((TM, w), lambda i: (i, 0))
    in_specs = [
        pl.BlockSpec((HALO, D_MODEL), lambda i: (jnp.maximum(i * hb - 1, 0), 0)),
        tile(D_MODEL),
        pl.BlockSpec((HALO, D_MODEL), lambda i: (jnp.minimum((i + 1) * hb, n // HALO - 1), 0)),
        _const((1, D_MODEL)), _const((D_MODEL, C_END)), _const((2 * N_CH, D_MODEL)),
        _const((N_META, 2 * W_M)), _const((1, 2 * LANES)), _const((2 * N_CH, LANES)),
        _const((CONV_K, 2 * W_M)), _const((1, W_A)), _const((1, W_A)), _const((W_A, W_A)),
    ]
    bf = lambda: jax.ShapeDtypeStruct((n, W_M), BF16)
    out_shape = (bf(), bf(), bf(), bf(), bf(),
                 jax.ShapeDtypeStruct((n, 2 * LANES), F32),
                 jax.ShapeDtypeStruct((2 * N_CH, n), F32),
                 bf(), bf(), bf(), bf(),
                 jax.ShapeDtypeStruct((nt, N_META, W_M), F32))
    out_specs = (tile(W_M),) * 5 + (
        tile(2 * LANES), pl.BlockSpec((2 * N_CH, TM), lambda i: (0, i)),
    ) + (tile(W_M),) * 4 + (
        pl.BlockSpec((1, N_META, W_M), lambda i: (i, 0, 0)),)
    return pl.pallas_call(
        functools.partial(_inproj_kernel, tiles_per_seq),
        grid=(nt,), in_specs=in_specs, out_specs=out_specs, out_shape=out_shape,
        scratch_shapes=[pltpu.VMEM((TM + 2 * HALO, D_MODEL), BF16),
                        pltpu.VMEM((TM + 2 * HALO, 2 * W_M), F32)],
        compiler_params=_params(("arbitrary",)), name="in_proj",
    )(x2d, x2d, x2d, p["ng"], p["w_all"], p["wgt"], mpre, p["bgr"], p["bgc"],
      p["cw"], p["qng"], p["kng"], p["bd"])


def _mlstm(q, k, v, g, gt, mk, mv, mg, nseq, t_len):
    n = q.shape[0]
    nblk = t_len // LC
    fwd = lambda b, j: (b * nblk + j, 0)
    bwd = lambda b, j: (b * nblk + nblk - 1 - j, 0)
    fwd_t = lambda b, j: (0, b * nblk + j)
    bwd_t = lambda b, j: (0, b * nblk + nblk - 1 - j)

    def blocks(im, im_t):
        return [pl.BlockSpec((LC, W_M), im)] * 3 + [
            pl.BlockSpec((LC, 2 * LANES), im), pl.BlockSpec((2 * N_CH, LC), im_t)]

    in_specs = blocks(fwd, fwd_t) + blocks(bwd, bwd_t) + [
        pl.BlockSpec((1, N_META, W_M), lambda b, j: (b * (t_len // TM), 0, 0)),
        pl.BlockSpec((N_META, W_M), lambda b, j: (0, 0)),
        pl.BlockSpec((N_META, 2 * LANES), lambda b, j: (0, 0)),
    ]
    out = jax.ShapeDtypeStruct((n, W_M), BF16)
    return pl.pallas_call(
        _mlstm_kernel, grid=(nseq, nblk), in_specs=in_specs,
        out_specs=(pl.BlockSpec((LC, W_M), fwd), pl.BlockSpec((LC, W_M), bwd)),
        out_shape=(out, out),
        scratch_shapes=[pltpu.VMEM((2, H_M, DH_M, 2 * LANES), F32),
                        pltpu.VMEM((2, 8, LANES), F32)],
        compiler_params=_params(("arbitrary", "arbitrary")), name="mlstm",
    )(q, k, v, g, gt, q, k, v, g, gt, mk, mv, mg)


def _attn_out(qa, za, ka, va, mka, mva, hf, hb, o, z, x2d, p, nseq, t_len):
    n = qa.shape[0]
    rows = t_len // GRID_W
    nb = t_len // TM
    tile = lambda w: pl.BlockSpec((TM, w), lambda b, j: (b * nb + j, 0))
    kv_bytes = 2 * 2 * t_len * W_A * 2
    kv_mode = {} if kv_bytes <= KV_DOUBLE_BUFFER_BYTES else dict(pipeline_mode=pl.Buffered(1))
    seq = pl.BlockSpec((t_len, W_A), lambda b, j: (b, 0), **kv_mode)
    in_specs = [tile(W_A), tile(W_A), seq, seq, _const((N_META, W_A)), _const((N_META, W_A)),
                _const(p["bias"].shape),
                tile(W_M), tile(W_M), tile(W_M), tile(W_M), tile(D_MODEL),
                _const((1, W_M)), _const((2 * W_M, D_MODEL))]
    return pl.pallas_call(
        functools.partial(_attn_out_kernel, rows), grid=(nseq, nb), in_specs=in_specs,
        out_specs=tile(D_MODEL), out_shape=jax.ShapeDtypeStruct((n, D_MODEL), F32),
        scratch_shapes=[pltpu.VMEM((TM, W_M + W_A), BF16)],
        compiler_params=_params(("arbitrary", "arbitrary")), name="attn_out",
    )(qa, za, ka, va, mka, mva, p["bias"], hf, hb, o, z, x2d, p["mng"], p["w_out"])


def _trunk(x, meta, p):
    nseq, t_len, _ = x.shape
    assert t_len % TM == 0 and t_len % LC == 0 and t_len // GRID_W >= WIN_H
    mpre, mv, mg, mka, mva = meta
    x2d = x.reshape(nseq * t_len, D_MODEL)
    q, k, v, o, z, g, gt, qa, ka, va, za, mk = _in_proj(x2d, t_len // TM, mpre, p)
    hf, hb = _mlstm(q, k, v, g, gt, mk, mv, mg, nseq, t_len)
    y = _attn_out(qa, za, ka, va, mka, mva, hf, hb, o, z, x2d, p, nseq, t_len)
    return y.reshape(nseq, t_len, D_MODEL)


def kernel(x_prompt, x_sample, meta_tokens, norm_g, w_in, b_gate, conv_w, mlstm_norm_g,
           q_norm_g, k_norm_g, rpb, w_out):
    assert norm_g.shape[0] == 1, "single-layer trunk"
    p = _prep_weights(norm_g[0], w_in[0], b_gate[0], conv_w[0], mlstm_norm_g[0],
                      q_norm_g[0], k_norm_g[0], rpb[0], w_out[0])
    meta = _meta_proj(meta_tokens, p)
    return _trunk(x_prompt, meta, p), _trunk(x_sample, meta, p)
```
